```python
import math
import jax, jax.numpy as jnp
from jax import lax
import numpy as np

D_MODEL = 2048
BATCH = 4
SEQ = 2048
DEPTH = 1
DEC_BATCH = 128
DEC_SEQ = 4
PAST_LEN = 16384
PAGE_SIZE = 128

MIX_WIDTH = D_MODEL
S5_WIDTH = MIX_WIDTH // 2
S5_GROUP_CH = 16
S5_GROUPS = S5_WIDTH // S5_GROUP_CH
S5_STATE = 64
HG_WIDTH = MIX_WIDTH - S5_WIDTH
HG_KEY_DIM = 128
HG_HEADS = HG_WIDTH // HG_KEY_DIM
HG_VAL_DIM = HG_WIDTH // HG_HEADS
HG_CHUNK = 32
IN_WIDTH = S5_WIDTH + 4 * HG_WIDTH
IN_SPLITS = (S5_WIDTH, S5_WIDTH + HG_WIDTH, S5_WIDTH + 2 * HG_WIDTH, S5_WIDTH + 3 * HG_WIDTH)
N_EXPERTS = 256
TOP_K = 8
N_EXPERT_GROUPS = 8
TOPK_GROUPS = 4
EXPERT_DIM = 512
SHARED_DIM = 512
ROUTED_SCALE = 2.5
MOE_BLOCK = 128
DEEPNORM_ALPHA = (2.0 * DEPTH) ** 0.25
DEEPNORM_BETA = (8.0 * DEPTH) ** -0.25
LN_EPS = 1e-5
RMS_EPS = 1e-6
DT_MIN = 1e-3
DT_MAX = 1e-1

kernel_name = "hymba_s5_hgrn2_moe_deepnorm_adaln_step"


def layer_norm(x):
    xf = x.astype(jnp.float32)
    mu = jnp.mean(xf, -1, keepdims=True)
    var = jnp.mean(jnp.square(xf - mu), -1, keepdims=True)
    return (xf - mu) * lax.rsqrt(var + LN_EPS)


def ada_modulation(c, w_ada, b_ada):
    m = jax.nn.silu(c.astype(jnp.float32)) @ w_ada.astype(jnp.float32) + b_ada.astype(jnp.float32)
    return jnp.split(m, 6, axis=-1)


def modulate(x, shift, scale):
    return (layer_norm(x) * (1.0 + scale[:, None, :]) + shift[:, None, :]).astype(x.dtype)


def post_norm(x, f, gate, ln_g, ln_b):
    y = DEEPNORM_ALPHA * x.astype(jnp.float32) + (1.0 + gate[:, None, :]) * f.astype(jnp.float32)
    return (layer_norm(y) * ln_g.astype(jnp.float32) + ln_b.astype(jnp.float32)).astype(x.dtype)


def s5_mixer(u, h0_re, h0_im, lam_re, lam_im, log_dt, b_re, b_im, c_re, c_im, d_skip, w_glu, b_glu):
    N, L, _ = u.shape
    f32 = jnp.float32
    uf = u.astype(f32).reshape(N, L, S5_GROUPS, S5_GROUP_CH)
    dt = jnp.exp(log_dt.astype(f32))[:, None]
    lr = jnp.minimum(lam_re.astype(f32), -1e-4)
    li = lam_im.astype(f32)
    mag = jnp.exp(lr * dt)
    a_re, a_im = mag * jnp.cos(li * dt), mag * jnp.sin(li * dt)
    den = lr * lr + li * li
    z_re, z_im = ((a_re - 1.0) * lr + a_im * li) / den, (a_im * lr - (a_re - 1.0) * li) / den
    bu_re = jnp.einsum('nlgc,gpc->nlgp', uf, b_re.astype(f32)) - 0.0 * 0.0
    bu_im = jnp.einsum('nlgc,gpc->nlgp', uf, b_im.astype(f32))
    x_re = z_re * bu_re - z_im * bu_im
    x_im = z_re * bu_im + z_im * bu_re

    def combine(e1, e2):
        a1r, a1i, b1r, b1i = e1
        a2r, a2i, b2r, b2i = e2
        return (a1r * a2r - a1i * a2i, a1r * a2i + a1i * a2r,
                a2r * b1r - a2i * b1i + b2r, a2r * b1i + a2i * b1r + b2i)

    elems = (jnp.broadcast_to(a_re, x_re.shape), jnp.broadcast_to(a_im, x_re.shape), x_re, x_im)
    _, _, h_re, h_im = lax.associative_scan(combine, elems, axis=1)
    if h0_re is not None:
        t1 = jnp.arange(1, L + 1, dtype=f32)[:, None, None]
        pm = jnp.exp(lr * dt * t1)
        p_re, p_im = pm * jnp.cos(li * dt * t1), pm * jnp.sin(li * dt * t1)
        s_re, s_im = h0_re.astype(f32)[:, None], h0_im.astype(f32)[:, None]
        h_re, h_im = h_re + p_re * s_re - p_im * s_im, h_im + p_re * s_im + p_im * s_re
    y = (jnp.einsum('nlgp,gcp->nlgc', h_re, c_re.astype(f32))
         - jnp.einsum('nlgp,gcp->nlgc', h_im, c_im.astype(f32))
         + d_skip.astype(f32) * uf)
    y = jax.nn.gelu(y).reshape(N, L, S5_WIDTH)
    y = y * jax.nn.sigmoid(y @ w_glu.astype(f32) + b_glu.astype(f32))
    return y.astype(u.dtype), h_re[:, -1], h_im[:, -1]


def gla_chunked(q, k, v, logg, s0):
    N, L, H, K = q.shape
    C = min(HG_CHUNK, L)
    pad = (-L) % C
    if pad:
        pw = ((0, 0), (0, pad), (0, 0), (0, 0))
        q, k, v, logg = [jnp.pad(t, pw) for t in (q, k, v, logg)]
    NC = (L + pad) // C
    q, k, v, logg = [t.reshape(N, NC, C, H, t.shape[-1]) for t in (q, k, v, logg)]
    b = jnp.cumsum(logg, axis=2)
    b_last = b[:, :, -1]
    q_dec = q * jnp.exp(b)
    k_dec = k * jnp.exp(-b)
    k_tail = k * jnp.exp(b_last[:, :, None] - b)
    causal = jnp.tril(jnp.ones((C, C), dtype=bool))
    scores = jnp.where(causal, jnp.einsum('ncthk,ncshk->nchts', q_dec, k_dec), 0.0)
    o_intra = jnp.einsum('nchts,ncshv->ncthv', scores, v)
    kv = jnp.einsum('ncshk,ncshv->nchkv', k_tail, v)

    def step(S, inp):
        dec, kv_c = inp
        return dec[..., None] * S + kv_c, S

    s_last, s_prev = lax.scan(step, s0, (jnp.moveaxis(jnp.exp(b_last), 1, 0), jnp.moveaxis(kv, 1, 0)))
    s_prev = jnp.moveaxis(s_prev, 0, 1)
    o = o_intra + jnp.einsum('ncthk,nchkv->ncthv', q_dec, s_prev)
    return o.reshape(N, NC * C, H, -1)[:, :L], s_last


def hgrn2_mixer(q, f, i, g, lb, gn_gain, s0):
    N, L, _ = q.shape
    f32 = jnp.float32
    qh = (jax.nn.silu(q.astype(f32)) * HG_KEY_DIM ** -0.5).reshape(N, L, HG_HEADS, HG_KEY_DIM)
    fg = lb + (1.0 - lb) * jax.nn.sigmoid(f.astype(f32))
    kh = (1.0 - fg).reshape(N, L, HG_HEADS, HG_KEY_DIM)
    logg = jnp.log(fg).reshape(N, L, HG_HEADS, HG_KEY_DIM)
    vh = i.astype(f32).reshape(N, L, HG_HEADS, HG_VAL_DIM)
    if s0 is None:
        s0 = jnp.zeros((N, HG_HEADS, HG_KEY_DIM, HG_VAL_DIM), f32)
    o, s_last = gla_chunked(qh, kh, vh, logg, s0.astype(f32))
    o = o * lax.rsqrt(jnp.mean(o * o, -1, keepdims=True) + RMS_EPS)
    o = o * gn_gain.astype(f32).reshape(HG_HEADS, HG_VAL_DIM)
    o = o.reshape(N, L, HG_WIDTH) * jax.nn.silu(g.astype(f32))
    return o.astype(q.dtype), s_last


def token_mixer(h, s5_h0_re, s5_h0_im, hg_s0, lb, w_in, lam_re, lam_im, log_dt, b_re, b_im,
                c_re, c_im, d_skip, w_glu, b_glu, hg_gain, w_out):
    proj = h @ w_in
    u, q, f, i, g = jnp.split(proj, IN_SPLITS, axis=-1)
    y_s5, s5_re, s5_im = s5_mixer(u, s5_h0_re, s5_h0_im, lam_re, lam_im, log_dt, b_re, b_im,
                                  c_re, c_im, d_skip, w_glu, b_glu)
    y_hg, hg_s = hgrn2_mixer(q, f, i, g, lb, hg_gain, hg_s0)
    y = jnp.concatenate([y_s5, y_hg], axis=-1) @ w_out
    return y, s5_re, s5_im, hg_s


def moe_ffn(h, w_router, router_bias, w_gate, w_up, w_down, ws_gate, ws_up, ws_down):
    T, D = h.shape
    f32 = jnp.float32
    scores = jax.nn.sigmoid((h @ w_router).astype(f32))
    biased = scores + router_bias.astype(f32)
    grouped = biased.reshape(T, N_EXPERT_GROUPS, N_EXPERTS // N_EXPERT_GROUPS)
    group_score = lax.top_k(grouped, 2)[0].sum(-1)
    _, top_groups = lax.top_k(group_score, TOPK_GROUPS)
    keep = jax.nn.one_hot(top_groups, N_EXPERT_GROUPS, dtype=f32).sum(1) > 0
    masked = jnp.where(keep[:, :, None], grouped, -jnp.inf).reshape(T, N_EXPERTS)
    _, expert_idx = lax.top_k(masked, TOP_K)
    gate = jnp.take_along_axis(scores, expert_idx, axis=1)
    gate = gate / jnp.sum(gate, -1, keepdims=True) * ROUTED_SCALE

    n_assign = T * TOP_K
    n_blocks = -(-(n_assign + N_EXPERTS * (MOE_BLOCK - 1)) // MOE_BLOCK)
    e_flat = expert_idx.reshape(-1)
    tok_flat = jnp.repeat(jnp.arange(T, dtype=jnp.int32), TOP_K)
    order = jnp.argsort(e_flat)
    e_s, tok_s, w_s = e_flat[order], tok_flat[order], gate.reshape(-1)[order]
    counts = jnp.bincount(e_flat, length=N_EXPERTS)
    starts = jnp.cumsum(counts) - counts
    padded = (counts + MOE_BLOCK - 1) // MOE_BLOCK * MOE_BLOCK
    pends = jnp.cumsum(padded)
    pstarts = pends - padded
    dest = pstarts[e_s] + (jnp.arange(n_assign) - starts[e_s])
    tok_buf = jnp.full((n_blocks * MOE_BLOCK,), T, jnp.int32).at[dest].set(tok_s)
    w_buf = jnp.zeros((n_blocks * MOE_BLOCK,), f32).at[dest].set(w_s)
    blk_expert = jnp.minimum(jnp.searchsorted(pends, jnp.arange(n_blocks) * MOE_BLOCK, side='right'), N_EXPERTS - 1)
    h_pad = jnp.concatenate([h, jnp.zeros((1, D), h.dtype)], axis=0)

    def expert_block(acc, blk):
        tok_b, w_b, e = blk
        xb = h_pad[tok_b]
        out = (jax.nn.silu(xb @ w_gate[e]) * (xb @ w_up[e])) @ w_down[e]
        return acc.at[tok_b].add(out.astype(f32) * w_b[:, None]), None

    acc, _ = lax.scan(expert_block, jnp.zeros((T + 1, D), f32),
                      (tok_buf.reshape(n_blocks, MOE_BLOCK), w_buf.reshape(n_blocks, MOE_BLOCK), blk_expert))
    shared = (jax.nn.silu(h @ ws_gate) * (h @ ws_up)) @ ws_down
    return (acc[:T] + shared.astype(f32)).astype(h.dtype)


def setup_inputs(seed: int = 0) -> dict:
    key = jax.random.key(seed)
    ks = iter(jax.random.split(key, 40))
    f32 = jnp.float32

    def nrm(shape, std):
        return jax.random.normal(next(ks), shape, f32) * std

    Dl, D, E, F = DEPTH, D_MODEL, N_EXPERTS, EXPERT_DIM
    G, P, Cg = S5_GROUPS, S5_STATE, S5_GROUP_CH
    return {
        "x_prompt": nrm((BATCH, SEQ, D), 1.0),
        "x_sample": nrm((DEC_BATCH, DEC_SEQ, D), 1.0),
        "c_prompt": nrm((BATCH, D), 1.0),
        "c_sample": nrm((DEC_BATCH, D), 1.0),
        "state_s5_re": nrm((Dl, DEC_BATCH, G, P), 0.2),
        "state_s5_im": nrm((Dl, DEC_BATCH, G, P), 0.2),
        "state_hgrn": nrm((Dl, DEC_BATCH, HG_HEADS, HG_KEY_DIM, HG_VAL_DIM), 0.5),
        "lb_logits": nrm((Dl + 1, HG_WIDTH), 0.5),
        "w_ada": nrm((Dl, D, 6 * D), 0.1 * D ** -0.5),
        "b_ada": nrm((Dl, 6 * D), 0.02),
        "w_in": nrm((Dl, D, IN_WIDTH), D ** -0.5),
        "s5_lam_re": -0.5 + nrm((Dl, G, P), 0.01),
        "s5_lam_im": jnp.pi * jnp.arange(P, dtype=f32) + nrm((Dl, G, P), 0.01),
        "s5_log_dt": jax.random.uniform(next(ks), (Dl, G), f32, math.log(DT_MIN), math.log(DT_MAX)),
        "s5_b_re": nrm((Dl, G, P, Cg), (2 * Cg) ** -0.5),
        "s5_b_im": nrm((Dl, G, P, Cg), (2 * Cg) ** -0.5),
        "s5_c_re": nrm((Dl, G, Cg, P), P ** -0.5),
        "s5_c_im": nrm((Dl, G, Cg, P), P ** -0.5),
        "s5_d": nrm((Dl, G, Cg), 1.0),
        "s5_w_glu": nrm((Dl, S5_WIDTH, S5_WIDTH), S5_WIDTH ** -0.5),
        "s5_b_glu": nrm((Dl, S5_WIDTH), 0.01),
        "hg_norm_gain": 1.0 + nrm((Dl, HG_WIDTH), 0.02),
        "w_out": nrm((Dl, MIX_WIDTH, D), MIX_WIDTH ** -0.5 * DEEPNORM_BETA),
        "ln1_g": 1.0 + nrm((Dl, D), 0.02),
        "ln1_b": nrm((Dl, D), 0.02),
        "w_router": nrm((Dl, D, E), D ** -0.5),
        "router_bias": nrm((Dl, E), 0.01),
        "w_exp_gate": nrm((Dl, E, D, F), D ** -0.5),
        "w_exp_up": nrm((Dl, E, D, F), D ** -0.5),
        "w_exp_down": nrm((Dl, E, F, D), F ** -0.5 * DEEPNORM_BETA),
        "w_sh_gate": nrm((Dl, D, SHARED_DIM), D ** -0.5),
        "w_sh_up": nrm((Dl, D, SHARED_DIM), D ** -0.5),
        "w_sh_down": nrm((Dl, SHARED_DIM, D), SHARED_DIM ** -0.5 * DEEPNORM_BETA),
        "ln2_g": 1.0 + nrm((Dl, D), 0.02),
        "ln2_b": nrm((Dl, D), 0.02),
    }


def reference(x_prompt, x_sample, c_prompt, c_sample, state_s5_re, state_s5_im, state_hgrn,
              lb_logits, w_ada, b_ada, w_in, s5_lam_re, s5_lam_im, s5_log_dt, s5_b_re, s5_b_im,
              s5_c_re, s5_c_im, s5_d, s5_w_glu, s5_b_glu, hg_norm_gain, w_out, ln1_g, ln1_b,
              w_router, router_bias, w_exp_gate, w_exp_up, w_exp_down, w_sh_gate, w_sh_up, w_sh_down,
              ln2_g, ln2_b):
    lb_all = jnp.cumsum(jax.nn.softmax(lb_logits.astype(jnp.float32), axis=0), axis=0)
    xp, xs = x_prompt, x_sample
    n_prompt_tok = xp.shape[0] * xp.shape[1]
    p_re, p_im, p_hg, s_re, s_im, s_hg = [], [], [], [], [], []
    for l in range(DEPTH):
        sh1p, sc1p, g1p, sh2p, sc2p, g2p = ada_modulation(c_prompt, w_ada[l], b_ada[l])
        sh1s, sc1s, g1s, sh2s, sc2s, g2s = ada_modulation(c_sample, w_ada[l], b_ada[l])
        mix_w = (lb_all[l], w_in[l], s5_lam_re[l], s5_lam_im[l], s5_log_dt[l], s5_b_re[l], s5_b_im[l],
                 s5_c_re[l], s5_c_im[l], s5_d[l], s5_w_glu[l], s5_b_glu[l], hg_norm_gain[l], w_out[l])
        mp, pre, pim, phg = token_mixer(modulate(xp, sh1p, sc1p), None, None, None, *mix_w)
        ms, sre, sim, shg = token_mixer(modulate(xs, sh1s, sc1s), state_s5_re[l], state_s5_im[l],
                                        state_hgrn[l], *mix_w)
        xp = post_norm(xp, mp, g1p, ln1_g[l], ln1_b[l])
        xs = post_norm(xs, ms, g1s, ln1_g[l], ln1_b[l])
        h2 = jnp.concatenate([modulate(xp, sh2p, sc2p).reshape(-1, D_MODEL),
                              modulate(xs, sh2s, sc2s).reshape(-1, D_MODEL)], axis=0)
        ff = moe_ffn(h2, w_router[l], router_bias[l], w_exp_gate[l], w_exp_up[l], w_exp_down[l],
                     w_sh_gate[l], w_sh_up[l], w_sh_down[l])
        xp = post_norm(xp, ff[:n_prompt_tok].reshape(xp.shape), g2p, ln2_g[l], ln2_b[l])
        xs = post_norm(xs, ff[n_prompt_tok:].reshape(xs.shape), g2s, ln2_g[l], ln2_b[l])
        p_re.append(pre); p_im.append(pim); p_hg.append(phg)
        s_re.append(sre); s_im.append(sim); s_hg.append(shg)
    return (xp, xs, jnp.stack(p_re), jnp.stack(p_im), jnp.stack(p_hg), jnp.stack(s_re), jnp.stack(s_im), jnp.stack(s_hg))
```

```python
import functools
import math

import jax
import jax.numpy as jnp
from jax import lax
from jax.experimental import pallas as pl
from jax.experimental.pallas import tpu as pltpu

F32 = jnp.float32
BF16 = jnp.bfloat16

D_MODEL = 2048
S5_WIDTH = 1024
S5_GROUP_CH = 16
S5_GROUPS = 64
S5_STATE = 64
S5_NSTATE = S5_GROUPS * S5_STATE
HG_WIDTH = 1024
HG_HEADS = 8
HG_DIM = 128
HG_CHUNK = 32
IN_WIDTH = S5_WIDTH + 4 * HG_WIDTH
N_EXPERTS = 256
TOP_K = 8
N_EXPERT_GROUPS = 8
TOPK_GROUPS = 4
EXPERT_DIM = 512
ROUTED_SCALE = 2.5
MOE_BLOCK = 128
DEEPNORM_ALPHA = 2.0 ** 0.25
LN_EPS = 1e-5
RMS_EPS = 1e-6

VMEM_LIMIT = 56 * 1024 * 1024
MXU_K = 256


def _params(sem, vmem=VMEM_LIMIT):
    return pltpu.CompilerParams(dimension_semantics=sem, vmem_limit_bytes=vmem)


def _layer_norm(x):
    mu = jnp.mean(x, axis=-1, keepdims=True)
    xc = x - mu
    var = jnp.mean(xc * xc, axis=-1, keepdims=True)
    return xc * lax.rsqrt(var + LN_EPS)


def _silu(x):
    return x * jax.nn.sigmoid(x)


def _dot(a, b):
    return jnp.dot(a, b, preferred_element_type=F32)


def _dot_nt(a, b):
    return lax.dot_general(a, b, (((1,), (1,)), ((), ())), preferred_element_type=F32)


def _ada_kernel(c_ref, w_ref, b_ref, o_ref):
    a = _silu(c_ref[...]).astype(BF16)
    o_ref[...] = _dot(a, w_ref[...].astype(BF16)) + b_ref[...]


def _ada_modulation(c_all, w_ada, b_ada):
    s, d = c_all.shape
    n = w_ada.shape[1]
    tn = 1024
    return pl.pallas_call(
        _ada_kernel,
        grid=(n // tn,),
        in_specs=[pl.BlockSpec((s, d), lambda j: (0, 0)),
                  pl.BlockSpec((d, tn), lambda j: (0, j)),
                  pl.BlockSpec((1, tn), lambda j: (0, j))],
        out_specs=pl.BlockSpec((s, tn), lambda j: (0, j)),
        out_shape=jax.ShapeDtypeStruct((s, n), F32),
        compiler_params=_params(("arbitrary",)),
        name="ada_modulation",
    )(c_all, w_ada, b_ada.reshape(1, n))


def _mod_spec(per_seq_rows, tm):
    if per_seq_rows:
        return pl.BlockSpec((None, 1, D_MODEL), lambda i, *_: (i // (per_seq_rows // tm), 0, 0))
    return pl.BlockSpec((tm, D_MODEL), lambda i, *_: (i, 0))


def _inproj_kernel(x_ref, sh_ref, sc_ref, w_ref, o_ref, h_scr):
    @pl.when(pl.program_id(1) == 0)
    def _():
        h = _layer_norm(x_ref[...]) * (1.0 + sc_ref[...]) + sh_ref[...]
        h_scr[...] = h.astype(BF16)

    o_ref[...] = _dot(h_scr[...], w_ref[...])


def _in_proj(x, shift, scale, w_in_bf, per_seq_rows, tm):
    m, d = x.shape
    n = w_in_bf.shape[1]
    tn = 1024
    return pl.pallas_call(
        _inproj_kernel,
        grid=(m // tm, n // tn),
        in_specs=[pl.BlockSpec((tm, d), lambda i, j: (i, 0)),
                  _mod_spec(per_seq_rows, tm), _mod_spec(per_seq_rows, tm),
                  pl.BlockSpec((d, tn), lambda i, j: (0, j))],
        out_specs=pl.BlockSpec((tm, tn), lambda i, j: (i, j)),
        out_shape=jax.ShapeDtypeStruct((m, n), F32),
        scratch_shapes=[pltpu.VMEM((tm, d), BF16)],
        compiler_params=_params(("arbitrary", "arbitrary")),
        name="in_proj",
    )(x, shift, scale, w_in_bf)


S5_LANES = 512
LANE = 128
S5_TILES = S5_LANES // LANE


def _xload(x, lg, rows):
    return jnp.concatenate([x[lg * S5_TILES + i, rows, :] for i in range(S5_TILES)], axis=1)


def _xstore(x, lg, rows, val):
    for i in range(S5_TILES):
        x[lg * S5_TILES + i, rows, :] = val[:, LANE * i:LANE * (i + 1)]


def _s5_input_map(u, bblk_ref, xre, xim):
    for c in range(4):
        ub = u[:, MXU_K * c:MXU_K * (c + 1)].astype(BF16)
        xb = _dot(ub, bblk_ref[c])
        for i in range(8):
            xre[8 * c + i] = xb[:, LANE * i:LANE * (i + 1)]
            xim[8 * c + i] = xb[:, 1024 + LANE * i:1024 + LANE * (i + 1)]


def _s5_output_map(u, xre, xim, cblk_ref, d_ref, wglu_ref, bglu_ref, y_ref):
    parts = []
    for c in range(4):
        hb = jnp.concatenate([xre[8 * c + i] for i in range(8)] + [xim[8 * c + i] for i in range(8)],
                             axis=1).astype(BF16)
        parts.append(_dot(hb, cblk_ref[c]))
    y = jnp.concatenate(parts, axis=1) + d_ref[...] * u
    y = jax.nn.gelu(y)
    y = y * jax.nn.sigmoid(_dot(y.astype(BF16), wglu_ref[...]) + bglu_ref[...])
    y_ref[...] = y.astype(y_ref.dtype)


def _s5_prompt_kernel(u_ref, bblk_ref, are_ref, aim_ref, pre_ref, pim_ref, cblk_ref, d_ref, wglu_ref,
                      bglu_ref, y_ref, fre_ref, fim_ref, xre, xim, car_re, car_im, cm_re, cm_im):
    rows_total = u_ref.shape[0]
    steps = rows_total // 8

    @pl.when(pl.program_id(1) == 0)
    def _():
        car_re[...] = jnp.zeros_like(car_re)
        car_im[...] = jnp.zeros_like(car_im)

    u = u_ref[...]
    _s5_input_map(u, bblk_ref, xre, xim)

    for lg in range(S5_NSTATE // S5_LANES):
        ls = slice(lg * S5_LANES, (lg + 1) * S5_LANES)
        ar = jnp.broadcast_to(are_ref[:, ls], (8, S5_LANES))
        ai = jnp.broadcast_to(aim_ref[:, ls], (8, S5_LANES))

        def local_step(j, h, lg=lg, ar=ar, ai=ai):
            hr, hi = h
            rows = pl.ds(j, 8, stride=steps)
            nr = ar * hr - ai * hi + _xload(xre, lg, rows)
            ni = ar * hi + ai * hr + _xload(xim, lg, rows)
            _xstore(xre, lg, rows, nr)
            _xstore(xim, lg, rows, ni)
            return nr, ni

        zero = jnp.zeros((8, S5_LANES), F32)
        hr, hi = lax.fori_loop(0, steps, local_step, (zero, zero))

        cr, ci = car_re[:, ls], car_im[:, ls]
        pjr, pji = pre_ref[steps - 1:steps, ls], pim_ref[steps - 1:steps, ls]
        for s in range(8):
            cm_re[s:s + 1, ls] = cr
            cm_im[s:s + 1, ls] = ci
            cr, ci = (pjr * cr - pji * ci + hr[s:s + 1], pjr * ci + pji * cr + hi[s:s + 1])
        car_re[:, ls] = cr
        car_im[:, ls] = ci
        cmr, cmi = cm_re[:, ls], cm_im[:, ls]

        def correct(j, carry, lg=lg, ls=ls, cmr=cmr, cmi=cmi):
            rows = pl.ds(j, 8, stride=steps)
            pr = pre_ref[pl.ds(j, 1), ls]
            pi = pim_ref[pl.ds(j, 1), ls]
            _xstore(xre, lg, rows, _xload(xre, lg, rows) + (pr * cmr - pi * cmi))
            _xstore(xim, lg, rows, _xload(xim, lg, rows) + (pr * cmi + pi * cmr))
            return carry

        lax.fori_loop(0, steps, correct, 0)

    fre_ref[...] = car_re[...]
    fim_ref[...] = car_im[...]
    _s5_output_map(u, xre, xim, cblk_ref, d_ref, wglu_ref, bglu_ref, y_ref)


def _s5_sample_kernel(u_ref, h0re_ref, h0im_ref, bblk_ref, are_ref, aim_ref, cblk_ref, d_ref, wglu_ref,
                      bglu_ref, y_ref, fre_ref, fim_ref, xre, xim, *, steps):
    n_seq = u_ref.shape[0] // steps
    u = u_ref[...]
    _s5_input_map(u, bblk_ref, xre, xim)

    for lg in range(S5_NSTATE // S5_LANES):
        ls = slice(lg * S5_LANES, (lg + 1) * S5_LANES)
        ar = jnp.broadcast_to(are_ref[:, ls], (8, S5_LANES))
        ai = jnp.broadcast_to(aim_ref[:, ls], (8, S5_LANES))
        for sg in range(n_seq // 8):
            hr = h0re_ref[8 * sg:8 * (sg + 1), ls]
            hi = h0im_ref[8 * sg:8 * (sg + 1), ls]
            for j in range(steps):
                rows = pl.ds(8 * sg * steps + j, 8, stride=steps)
                hr, hi = (ar * hr - ai * hi + _xload(xre, lg, rows), ar * hi + ai * hr + _xload(xim, lg, rows))
                _xstore(xre, lg, rows, hr)
                _xstore(xim, lg, rows, hi)
            fre_ref[8 * sg:8 * (sg + 1), ls] = hr
            fim_ref[8 * sg:8 * (sg + 1), ls] = hi

    _s5_output_map(u, xre, xim, cblk_ref, d_ref, wglu_ref, bglu_ref, y_ref)


def _s5_weights(lam_re, lam_im, log_dt, b_re, b_im, c_re, c_im, d_skip, n_pow):
    g, p, cg = S5_GROUPS, S5_STATE, S5_GROUP_CH
    dt = jnp.exp(log_dt.astype(F32))[:, None]
    lr = jnp.minimum(lam_re.astype(F32), -1e-4)
    li = lam_im.astype(F32)
    mag = jnp.exp(lr * dt)
    a_re, a_im = mag * jnp.cos(li * dt), mag * jnp.sin(li * dt)
    den = lr * lr + li * li
    z_re = ((a_re - 1.0) * lr + a_im * li) / den
    z_im = (a_im * lr - (a_re - 1.0) * li) / den
    zb_re = z_re[:, :, None] * b_re.astype(F32) - z_im[:, :, None] * b_im.astype(F32)
    zb_im = z_re[:, :, None] * b_im.astype(F32) + z_im[:, :, None] * b_re.astype(F32)
    eye = jnp.eye(16, dtype=F32)

    def in_blocks(w):
        w = w.reshape(4, 16, p, cg)
        blk = jnp.einsum('agpc,gh->agchp', w, eye)
        return blk.reshape(4, 16 * cg, 16 * p)

    def out_blocks(w):
        w = w.reshape(4, 16, cg, p)
        blk = jnp.einsum('agcp,gh->agphc', w, eye)
        return blk.reshape(4, 16 * p, 16 * cg)

    bblk = jnp.concatenate([in_blocks(zb_re), in_blocks(zb_im)], axis=2).astype(BF16)
    cblk = jnp.concatenate([out_blocks(c_re.astype(F32)), -out_blocks(c_im.astype(F32))], axis=1).astype(BF16)
    t1 = jnp.arange(1, n_pow + 1, dtype=F32)[:, None, None]
    pm = jnp.exp(lr * dt * t1)
    pow_re = (pm * jnp.cos(li * dt * t1)).reshape(n_pow, g * p)
    pow_im = (pm * jnp.sin(li * dt * t1)).reshape(n_pow, g * p)
    return dict(bblk=bblk, cblk=cblk, a_re=a_re.reshape(1, g * p), a_im=a_im.reshape(1, g * p),
                pow_re=pow_re, pow_im=pow_im, d=d_skip.astype(F32).reshape(1, g * cg))


S5_PROMPT_ROWS = 256


def _s5_prompt(proj, n_seq, seq_len, sw, wglu_bf, b_glu):
    r = S5_PROMPT_ROWS
    nblk = seq_len // r
    const2 = lambda n, c: (0, 0)
    const3 = lambda n, c: (0, 0, 0)
    ns = S5_NSTATE
    y, fre, fim = pl.pallas_call(
        _s5_prompt_kernel,
        grid=(n_seq, nblk),
        in_specs=[pl.BlockSpec((r, S5_WIDTH), lambda n, c: (n * nblk + c, 0)),
                  pl.BlockSpec((4, MXU_K, 2048), const3),
                  pl.BlockSpec((1, ns), const2), pl.BlockSpec((1, ns), const2),
                  pl.BlockSpec((r // 8, ns), const2), pl.BlockSpec((r // 8, ns), const2),
                  pl.BlockSpec((4, 2048, MXU_K), const3),
                  pl.BlockSpec((1, S5_WIDTH), const2),
                  pl.BlockSpec((S5_WIDTH, S5_WIDTH), const2),
                  pl.BlockSpec((1, S5_WIDTH), const2)],
        out_specs=[pl.BlockSpec((r, S5_WIDTH), lambda n, c: (n * nblk + c, 0)),
                   pl.BlockSpec((None, 1, ns), lambda n, c: (n, 0, 0)),
                   pl.BlockSpec((None, 1, ns), lambda n, c: (n, 0, 0))],
        out_shape=[jax.ShapeDtypeStruct((n_seq * seq_len, S5_WIDTH), BF16),
                   jax.ShapeDtypeStruct((n_seq, 1, ns), F32),
                   jax.ShapeDtypeStruct((n_seq, 1, ns), F32)],
        scratch_shapes=[pltpu.VMEM((ns // LANE, r, LANE), F32), pltpu.VMEM((ns // LANE, r, LANE), F32),
                        pltpu.VMEM((1, ns), F32), pltpu.VMEM((1, ns), F32),
                        pltpu.VMEM((8, ns), F32), pltpu.VMEM((8, ns), F32)],
        compiler_params=_params(("arbitrary", "arbitrary")),
        name="s5_prompt",
    )(proj, sw["bblk"], sw["a_re"], sw["a_im"], sw["pow_re"], sw["pow_im"], sw["cblk"], sw["d"],
      wglu_bf, b_glu)
    return y, fre, fim


S5_SAMPLE_SEQS = 32


def _s5_sample(proj, h0_re, h0_im, n_seq, steps, sw, wglu_bf, b_glu):
    sb = S5_SAMPLE_SEQS
    r = sb * steps
    const2 = lambda i: (0, 0)
    const3 = lambda i: (0, 0, 0)
    ns = S5_NSTATE
    return pl.pallas_call(
        functools.partial(_s5_sample_kernel, steps=steps),
        grid=(n_seq // sb,),
        in_specs=[pl.BlockSpec((r, S5_WIDTH), lambda i: (i, 0)),
                  pl.BlockSpec((sb, ns), lambda i: (i, 0)), pl.BlockSpec((sb, ns), lambda i: (i, 0)),
                  pl.BlockSpec((4, MXU_K, 2048), const3),
                  pl.BlockSpec((1, ns), const2), pl.BlockSpec((1, ns), const2),
                  pl.BlockSpec((4, 2048, MXU_K), const3),
                  pl.BlockSpec((1, S5_WIDTH), const2),
                  pl.BlockSpec((S5_WIDTH, S5_WIDTH), const2),
                  pl.BlockSpec((1, S5_WIDTH), const2)],
        out_specs=[pl.BlockSpec((r, S5_WIDTH), lambda i: (i, 0)),
                   pl.BlockSpec((sb, ns), lambda i: (i, 0)),
                   pl.BlockSpec((sb, ns), lambda i: (i, 0))],
        out_shape=[jax.ShapeDtypeStruct((n_seq * steps, S5_WIDTH), BF16),
                   jax.ShapeDtypeStruct((n_seq, ns), F32),
                   jax.ShapeDtypeStruct((n_seq, ns), F32)],
        scratch_shapes=[pltpu.VMEM((ns // LANE, r, LANE), F32), pltpu.VMEM((ns // LANE, r, LANE), F32)],
        compiler_params=_params(("arbitrary",)),
        name="s5_sample",
    )(proj, h0_re, h0_im, sw["bblk"], sw["a_re"], sw["a_im"], sw["cblk"], sw["d"], wglu_bf, b_glu)


def _split3(x):
    hi = x.astype(BF16)
    r1 = x - hi.astype(F32)
    mid = r1.astype(BF16)
    lo = (r1 - mid.astype(F32)).astype(BF16)
    return hi, mid, lo


def _hgrn_kernel(q_ref, f_ref, i_ref, g_ref, lb_ref, gain_ref, tri_ref, *rest, chunk, chained):
    if chained:
        o_ref, sfin_ref, st_scr = rest
    else:
        s0_ref, o_ref, sfin_ref = rest
    rows = q_ref.shape[0]
    n_chunks = rows // chunk
    dim = HG_DIM

    lb = lb_ref[...]
    qs = _silu(q_ref[...]) * (dim ** -0.5)
    fg = lb + (1.0 - lb) * jax.nn.sigmoid(f_ref[...])
    k = 1.0 - fg
    logg = jnp.log(fg)
    v = i_ref[...]

    hi, mid, lo = _split3(logg)
    bb = _dot(tri_ref[...], jnp.concatenate([hi, mid, lo], axis=1))
    bsum = (bb[:, 2 * dim:] + bb[:, dim:2 * dim]) + bb[:, :dim]
    b, btot = bsum[:rows], bsum[rows:]

    q_dec_f = qs * jnp.exp(b)
    q_dec = q_dec_f.astype(BF16)
    k_dec = (k * jnp.exp(-b)).astype(BF16)
    k_tail = (k * jnp.exp(btot - b)).astype(BF16)
    v_bf = v.astype(BF16)

    causal = tri_ref[:rows, :].astype(F32) > 0.0
    scores = jnp.where(causal, _dot_nt(q_dec, k_dec), 0.0)
    o = _dot(scores.astype(BF16), v_bf)

    vt = v.T
    lane_id = lax.broadcasted_iota(jnp.int32, (dim, rows), 1)
    row_id = lax.broadcasted_iota(jnp.int32, (rows, dim), 0)

    def in_chunk(ids, c):
        return (ids >= c * chunk) & (ids < (c + 1) * chunk)

    if chained:
        @pl.when(pl.program_id(2) == 0)
        def _():
            st_scr[...] = jnp.zeros_like(st_scr)
        st = st_scr[...]

    o_inter = [] if chunk % 8 == 0 else jnp.zeros((rows, dim), F32)
    for c in range(n_chunks):
        if not chained:
            st = s0_ref[c, 0].T
        st_bf = st.astype(BF16)
        if chunk % 8 == 0:
            o_inter.append(_dot_nt(q_dec[c * chunk:(c + 1) * chunk], st_bf))
        else:
            qm = jnp.where(in_chunk(row_id, c), q_dec_f, 0.0).astype(BF16)
            o_inter = o_inter + _dot_nt(qm, st_bf)
        vtm = jnp.where(in_chunk(lane_id, c), vt, 0.0).astype(BF16)
        kv_t = _dot(vtm, k_tail)
        dec = jnp.exp(btot[c * chunk:c * chunk + 1, :])
        st = st * dec + kv_t
        if not chained:
            sfin_ref[c, 0] = st.T
    if chunk % 8 == 0:
        o_inter = jnp.concatenate(o_inter, axis=0)
    if chained:
        st_scr[...] = st
        sfin_ref[...] = st.T

    o = o + o_inter
    o = o * lax.rsqrt(jnp.mean(o * o, axis=-1, keepdims=True) + RMS_EPS)
    o = o * gain_ref[...] * _silu(g_ref[...])
    o_ref[...] = o.astype(o_ref.dtype)


def _chunk_matrix(rows, chunk):
    t = jnp.arange(rows)
    same = (t[:, None] // chunk) == (t[None, :] // chunk)
    causal = same & (t[None, :] <= t[:, None])
    return jnp.concatenate([causal, same], axis=0).astype(BF16)


HG_PROMPT_ROWS = 256


def _hgrn_prompt(proj, lb, gain, n_seq, seq_len):
    r = HG_PROMPT_ROWS
    nblk = seq_len // r
    nh = HG_HEADS

    def col(which):
        return pl.BlockSpec((r, HG_DIM), lambda n, h, c: (n * nblk + c, (1 + which) * nh + h))

    vec = pl.BlockSpec((1, HG_DIM), lambda n, h, c: (0, h))
    return pl.pallas_call(
        functools.partial(_hgrn_kernel, chunk=HG_CHUNK, chained=True),
        grid=(n_seq, nh, nblk),
        in_specs=[col(0), col(1), col(2), col(3), vec, vec,
                  pl.BlockSpec((2 * r, r), lambda n, h, c: (0, 0))],
        out_specs=[pl.BlockSpec((r, HG_DIM), lambda n, h, c: (n * nblk + c, h)),
                   pl.BlockSpec((None, None, HG_DIM, HG_DIM), lambda n, h, c: (n, h, 0, 0))],
        out_shape=[jax.ShapeDtypeStruct((n_seq * seq_len, HG_WIDTH), BF16),
                   jax.ShapeDtypeStruct((n_seq, nh, HG_DIM, HG_DIM), F32)],
        scratch_shapes=[pltpu.VMEM((HG_DIM, HG_DIM), F32)],
        compiler_params=_params(("arbitrary", "arbitrary", "arbitrary")),
        name="hgrn_prompt",
    )(proj, proj, proj, proj, lb, gain, _chunk_matrix(r, HG_CHUNK))


HG_SAMPLE_SEQS = 32


def _hgrn_sample(proj, s0, lb, gain, n_seq, steps):
    sb = HG_SAMPLE_SEQS
    r = sb * steps
    nh = HG_HEADS

    def col(which):
        return pl.BlockSpec((r, HG_DIM), lambda i, h: (i, (1 + which) * nh + h))

    vec = pl.BlockSpec((1, HG_DIM), lambda i, h: (0, h))
    state = pl.BlockSpec((sb, 1, HG_DIM, HG_DIM), lambda i, h: (i, h, 0, 0))
    return pl.pallas_call(
        functools.partial(_hgrn_kernel, chunk=steps, chained=False),
        grid=(n_seq // sb, nh),
        in_specs=[col(0), col(1), col(2), col(3), vec, vec,
                  pl.BlockSpec((2 * r, r), lambda i, h: (0, 0)), state],
        out_specs=[pl.BlockSpec((r, HG_DIM), lambda i, h: (i, h)), state],
        out_shape=[jax.ShapeDtypeStruct((n_seq * steps, HG_WIDTH), BF16),
                   jax.ShapeDtypeStruct((n_seq, nh, HG_DIM, HG_DIM), F32)],
        compiler_params=_params(("arbitrary", "arbitrary")),
        name="hgrn_sample",
    )(proj, proj, proj, proj, lb, gain, _chunk_matrix(r, steps), s0)


def _outproj_kernel(ys_ref, yh_ref, x_ref, g1_ref, sh2_ref, sc2_ref, wo_s_ref, wo_h_ref, lng_ref, lnb_ref,
                    wr_hi_ref, wr_lo_ref, x1_ref, h2_ref, lg_ref):
    f = _dot(ys_ref[...], wo_s_ref[...]) + _dot(yh_ref[...], wo_h_ref[...])
    y = DEEPNORM_ALPHA * x_ref[...] + (1.0 + g1_ref[...]) * f
    x1 = _layer_norm(y) * lng_ref[...] + lnb_ref[...]
    x1_ref[...] = x1
    h2 = _layer_norm(x1) * (1.0 + sc2_ref[...]) + sh2_ref[...]
    h2_ref[...] = h2
    h_hi = h2.astype(BF16)
    h_lo = (h2 - h_hi.astype(F32)).astype(BF16)
    lg = _dot_nt(wr_hi_ref[...], h_hi) + (_dot_nt(wr_hi_ref[...], h_lo) + _dot_nt(wr_lo_ref[...], h_hi))
    lg_ref[...] = lg


def _out_proj(y_s5, y_hg, x, g1, sh2, sc2, wo_s, wo_h, ln_g, ln_b, wr_hi, wr_lo, per_seq_rows, tm,
              row_offset, h2_all, total_rows):
    m, d = x.shape
    half = y_s5.shape[1]
    off = row_offset // tm
    row = lambda i: (i, 0)
    const = lambda i: (0, 0)
    in_specs = [pl.BlockSpec((tm, half), row), pl.BlockSpec((tm, half), row), pl.BlockSpec((tm, d), row),
                _mod_spec(per_seq_rows, tm), _mod_spec(per_seq_rows, tm), _mod_spec(per_seq_rows, tm),
                pl.BlockSpec((half, d), const), pl.BlockSpec((half, d), const),
                pl.BlockSpec((1, d), const), pl.BlockSpec((1, d), const),
                pl.BlockSpec((N_EXPERTS, d), const), pl.BlockSpec((N_EXPERTS, d), const)]
    args = [y_s5, y_hg, x, g1, sh2, sc2, wo_s, wo_h, ln_g, ln_b, wr_hi, wr_lo]
    aliases = {}
    kernel = _outproj_kernel
    if h2_all is not None:
        in_specs.append(pl.BlockSpec(memory_space=pl.ANY))
        args.append(h2_all)
        aliases = {len(args) - 1: 1}
        kernel = lambda *refs: _outproj_kernel(*refs[:12], *refs[13:])
    return pl.pallas_call(
        kernel,
        grid=(m // tm,),
        in_specs=in_specs,
        out_specs=[pl.BlockSpec((tm, d), row),
                   pl.BlockSpec((tm, d), lambda i: (i + off, 0)),
                   pl.BlockSpec((N_EXPERTS, tm), lambda i: (0, i))],
        out_shape=[jax.ShapeDtypeStruct((m, d), F32),
                   jax.ShapeDtypeStruct((total_rows, d), F32),
                   jax.ShapeDtypeStruct((N_EXPERTS, m), F32)],
        input_output_aliases=aliases,
        compiler_params=_params(("arbitrary",)),
        name="out_proj",
    )(*args)


def _router_kernel(lg_ref, bias_ref, idx_ref, gate_ref):
    ne, tt = lg_ref.shape
    ng = N_EXPERT_GROUPS
    gs = ne // ng
    neg = -jnp.inf
    scores = jax.nn.sigmoid(lg_ref[...])
    biased = scores + bias_ref[...]
    in_grp = lax.broadcasted_iota(jnp.int32, (gs, tt), 0).astype(F32)
    gscores = []
    for g in range(ng):
        grp = biased[g * gs:(g + 1) * gs]
        m1 = jnp.max(grp, axis=0, keepdims=True)
        i1 = jnp.min(jnp.where(grp == m1, in_grp, float(gs)), axis=0, keepdims=True)
        m2 = jnp.max(jnp.where(in_grp == i1, neg, grp), axis=0, keepdims=True)
        gscores.append(m1 + m2)
    gscore = jnp.concatenate(gscores, axis=0)
    grp_id = lax.broadcasted_iota(jnp.int32, (ng, tt), 0).astype(F32)
    keep = jnp.zeros((ng, tt), F32)
    for _ in range(TOPK_GROUPS):
        gm = jnp.max(gscore, axis=0, keepdims=True)
        gi = jnp.min(jnp.where(gscore == gm, grp_id, float(ng)), axis=0, keepdims=True)
        sel = grp_id == gi
        keep = jnp.where(sel, 1.0, keep)
        gscore = jnp.where(sel, neg, gscore)
    masked = jnp.concatenate(
        [jnp.where(keep[g:g + 1] > 0.0, biased[g * gs:(g + 1) * gs], neg) for g in range(ng)], axis=0)
    eid = lax.broadcasted_iota(jnp.int32, (ne, tt), 0).astype(F32)
    idxs, gates = [], []
    for _ in range(TOP_K):
        m = jnp.max(masked, axis=0, keepdims=True)
        i = jnp.min(jnp.where(masked == m, eid, float(ne)), axis=0, keepdims=True)
        sel = eid == i
        idxs.append(i)
        gates.append(jnp.sum(jnp.where(sel, scores, 0.0), axis=0, keepdims=True))
        masked = jnp.where(sel, neg, masked)
    gate = jnp.concatenate(gates, axis=0)
    gate = gate / jnp.sum(gate, axis=0, keepdims=True) * ROUTED_SCALE
    idx_ref[...] = jnp.concatenate(idxs, axis=0).astype(jnp.int32)
    gate_ref[...] = gate


def _router(logits_t, bias):
    ne, t = logits_t.shape
    tt = 512
    return pl.pallas_call(
        _router_kernel,
        grid=(t // tt,),
        in_specs=[pl.BlockSpec((ne, tt), lambda i: (0, i)), pl.BlockSpec((ne, 1), lambda i: (0, 0))],
        out_specs=[pl.BlockSpec((TOP_K, tt), lambda i: (0, i)), pl.BlockSpec((TOP_K, tt), lambda i: (0, i))],
        out_shape=[jax.ShapeDtypeStruct((TOP_K, t), jnp.int32), jax.ShapeDtypeStruct((TOP_K, t), F32)],
        compiler_params=_params(("arbitrary",)),
        name="router_topk",
    )(logits_t, bias.astype(F32).reshape(ne, 1))


def _moe_kernel(blk_expert_ref, nact_ref, tok0_ref, tokn_ref, dst_ref, h_hbm, wg_ref, wu_ref, wd_ref,
                y_hbm, xbuf, ybuf, wg_bf, wu_bf, wd_bf, gsem, ssem):
    b = pl.program_id(0)
    nact = nact_ref[0]
    bm = MOE_BLOCK
    slot = b % 2

    def gather(tok_ref, to_slot):
        def issue(r, carry):
            t = tok_ref[0, 0, r]
            pltpu.make_async_copy(h_hbm.at[pl.ds(t, 1)], xbuf.at[to_slot, pl.ds(r, 1)],
                                  gsem.at[to_slot]).start()
            return carry
        lax.fori_loop(0, bm, issue, 0)

    def gather_wait(s):
        pltpu.make_async_copy(h_hbm.at[pl.ds(0, bm)], xbuf.at[s], gsem.at[s]).wait()

    def scatter_wait(s):
        pltpu.make_async_copy(ybuf.at[s], y_hbm.at[pl.ds(0, bm)], ssem.at[s]).wait()

    @pl.when((b == 0) & (nact > 0))
    def _():
        gather(tok0_ref, 0)

    @pl.when(b + 1 < nact)
    def _():
        gather(tokn_ref, 1 - slot)

    @pl.when((b >= 2) & (b - 2 < nact))
    def _():
        scatter_wait(slot)

    @pl.when(b < nact)
    def _():
        first = jnp.logical_or(b == 0, blk_expert_ref[b] != blk_expert_ref[jnp.maximum(b - 1, 0)])

        @pl.when(first)
        def _():
            wg_bf[...] = wg_ref[...].astype(BF16)
            wu_bf[...] = wu_ref[...].astype(BF16)
            wd_bf[...] = wd_ref[...].astype(BF16)

        gather_wait(slot)
        x = xbuf[slot].astype(BF16)
        hmid = _silu(_dot(x, wg_bf[...])) * _dot(x, wu_bf[...])
        ybuf[slot] = _dot(hmid.astype(BF16), wd_bf[...])

        def issue(r, carry):
            d = dst_ref[0, 0, r]
            pltpu.make_async_copy(ybuf.at[slot, pl.ds(r, 1)], y_hbm.at[pl.ds(d, 1)], ssem.at[slot]).start()
            return carry
        lax.fori_loop(0, bm, issue, 0)


def _moe_experts(h2, w_gate, w_up, w_down, blk_expert, nact, tok, dst, n_blocks):
    t, d = h2.shape
    bm = MOE_BLOCK
    f = w_gate.shape[2]
    y_rows = t * TOP_K + 2 * bm
    last = n_blocks - 1

    def wmap(b, be, na):
        return (be[jnp.minimum(b, last)], 0, 0)

    grid_spec = pltpu.PrefetchScalarGridSpec(
        num_scalar_prefetch=2,
        grid=(n_blocks + 2,),
        in_specs=[pl.BlockSpec((1, 1, bm), lambda b, be, na: (0, 0, 0), memory_space=pltpu.SMEM),
                  pl.BlockSpec((1, 1, bm), lambda b, be, na: (jnp.minimum(b + 1, last), 0, 0),
                               memory_space=pltpu.SMEM),
                  pl.BlockSpec((1, 1, bm), lambda b, be, na: (jnp.minimum(b, last), 0, 0),
                               memory_space=pltpu.SMEM),
                  pl.BlockSpec(memory_space=pl.ANY),
                  pl.BlockSpec((None, d, f), wmap), pl.BlockSpec((None, d, f), wmap),
                  pl.BlockSpec((None, f, d), wmap)],
        out_specs=pl.BlockSpec(memory_space=pl.ANY),
        scratch_shapes=[pltpu.VMEM((2, bm, d), F32), pltpu.VMEM((2, bm, d), F32),
                        pltpu.VMEM((d, f), BF16), pltpu.VMEM((d, f), BF16), pltpu.VMEM((f, d), BF16),
                        pltpu.SemaphoreType.DMA((2,)), pltpu.SemaphoreType.DMA((2,))],
    )
    return pl.pallas_call(
        _moe_kernel,
        grid_spec=grid_spec,
        out_shape=jax.ShapeDtypeStruct((y_rows, d), F32),
        compiler_params=_params(("arbitrary",)),
        name="moe_experts",
    )(blk_expert, nact, tok, tok, dst, h2, w_gate, w_up, w_down)


def _dispatch(expert_idx, n_tok, n_blocks):
    bm = MOE_BLOCK
    n_assign = n_tok * TOP_K
    e_flat = expert_idx.reshape(-1)
    order = jnp.argsort(e_flat).astype(jnp.int32)
    e_s = e_flat[order]
    counts = jnp.bincount(e_flat, length=N_EXPERTS).astype(jnp.int32)
    starts = jnp.cumsum(counts) - counts
    padded = (counts + bm - 1) // bm * bm
    pends = jnp.cumsum(padded)
    pstarts = pends - padded
    dest = pstarts[e_s] + (jnp.arange(n_assign, dtype=jnp.int32) - starts[e_s])
    asg = jnp.full((n_blocks * bm,), -1, jnp.int32).at[dest].set(order)
    pos = jnp.arange(n_blocks * bm, dtype=jnp.int32)
    valid = asg >= 0
    tok = jnp.where(valid, asg // TOP_K, 0)
    trash = n_assign + ((pos // bm) % 2) * bm + pos % bm
    dst = jnp.where(valid, (asg % TOP_K) * n_tok + asg // TOP_K, trash)
    blk_id = jnp.arange(n_blocks, dtype=jnp.int32)
    nact = (pends[-1] // bm).astype(jnp.int32)
    blk_expert = jnp.minimum(
        jnp.searchsorted(pends, jnp.minimum(blk_id, nact - 1) * bm, side='right'),
        N_EXPERTS - 1).astype(jnp.int32)
    nact = nact.reshape(1)
    return (blk_expert, nact, tok.reshape(n_blocks, 1, bm), dst.reshape(n_blocks, 1, bm))


def _combine_kernel(*refs):
    (h_ref, x_ref, g2_ref, gate_ref, wsg_ref, wsu_ref, wsd_ref, lng_ref, lnb_ref), ys, o_ref = (
        refs[:9], refs[9:9 + TOP_K], refs[9 + TOP_K])
    h = h_ref[...].astype(BF16)
    mid = _silu(_dot(h, wsg_ref[...])) * _dot(h, wsu_ref[...])
    ff = _dot(mid.astype(BF16), wsd_ref[...])
    gate = gate_ref[...]
    for kk in range(TOP_K):
        ff = ff + gate[:, kk:kk + 1] * ys[kk][...]
    y = DEEPNORM_ALPHA * x_ref[...] + (1.0 + g2_ref[...]) * ff
    o_ref[...] = _layer_norm(y) * lng_ref[...] + lnb_ref[...]


def _combine(h2_all, x1, g2, gate, ybuf, wsg, wsu, wsd, ln_g, ln_b, per_seq_rows, tm, row_offset, total_rows):
    m, d = x1.shape
    off = row_offset // tm
    const = lambda i: (0, 0)
    sf = wsg.shape[1]
    in_specs = [pl.BlockSpec((tm, d), lambda i: (i + off, 0)),
                pl.BlockSpec((tm, d), lambda i: (i, 0)),
                _mod_spec(per_seq_rows, tm),
                pl.BlockSpec((tm, TOP_K), lambda i: (i + off, 0)),
                pl.BlockSpec((d, sf), const), pl.BlockSpec((d, sf), const), pl.BlockSpec((sf, d), const),
                pl.BlockSpec((1, d), const), pl.BlockSpec((1, d), const)]
    for kk in range(TOP_K):
        in_specs.append(pl.BlockSpec((tm, d), lambda i, kk=kk: (kk * (total_rows // tm) + off + i, 0)))
    return pl.pallas_call(
        _combine_kernel,
        grid=(m // tm,),
        in_specs=in_specs,
        out_specs=pl.BlockSpec((tm, d), lambda i: (i, 0)),
        out_shape=jax.ShapeDtypeStruct((m, d), F32),
        compiler_params=_params(("arbitrary",)),
        name="combine",
    )(h2_all, x1, g2, gate, wsg, wsu, wsd, ln_g, ln_b, *([ybuf] * TOP_K))


def kernel(x_prompt, x_sample, c_prompt, c_sample, state_s5_re, state_s5_im, state_hgrn, lb_logits, w_ada, b_ada, w_in, s5_lam_re, s5_lam_im, s5_log_dt, s5_b_re, s5_b_im, s5_c_re, s5_c_im, s5_d, s5_w_glu, s5_b_glu, hg_norm_gain, w_out, ln1_g, ln1_b, w_router, router_bias, w_exp_gate, w_exp_up, w_exp_down, w_sh_gate, w_sh_up, w_sh_down, ln2_g, ln2_b):
    n_p, seq, d = x_prompt.shape
    n_s, steps, _ = x_sample.shape
    tp, ts = n_p * seq, n_s * steps
    t_all = tp + ts
    depth = w_in.shape[0]
    assert depth == 1 and d == D_MODEL
    l = 0

    lb_all = jnp.cumsum(jax.nn.softmax(lb_logits.astype(F32), axis=0), axis=0)
    lb = lb_all[l].reshape(1, HG_WIDTH)
    gain = hg_norm_gain[l].astype(F32).reshape(1, HG_WIDTH)

    n_c = n_p + n_s
    c_pad = (-n_c) % 8
    c_all = jnp.concatenate([c_prompt, c_sample, jnp.zeros((c_pad, d), F32)], axis=0)
    mod = _ada_modulation(c_all, w_ada[l], b_ada[l])
    mods = [mod[:, i * d:(i + 1) * d] for i in range(6)]
    mod_p = [m[:n_p].reshape(n_p, 1, d) for m in mods]
    mod_s = [jnp.repeat(m[n_p:n_c], steps, axis=0) for m in mods]

    w_in_bf = w_in[l].astype(BF16)
    wglu_bf = s5_w_glu[l].astype(BF16)
    b_glu = s5_b_glu[l].astype(F32).reshape(1, S5_WIDTH)
    wo_bf = w_out[l].astype(BF16)
    wo_s, wo_h = wo_bf[:S5_WIDTH], wo_bf[S5_WIDTH:]
    wr_t = w_router[l].astype(F32).T
    wr_hi = wr_t.astype(BF16)
    wr_lo = (wr_t - wr_hi.astype(F32)).astype(BF16)
    ln1g, ln1b = ln1_g[l].astype(F32).reshape(1, d), ln1_b[l].astype(F32).reshape(1, d)
    ln2g, ln2b = ln2_g[l].astype(F32).reshape(1, d), ln2_b[l].astype(F32).reshape(1, d)

    sw = _s5_weights(s5_lam_re[l], s5_lam_im[l], s5_log_dt[l], s5_b_re[l], s5_b_im[l], s5_c_re[l],
                     s5_c_im[l], s5_d[l], S5_PROMPT_ROWS // 8)

    xp = x_prompt.reshape(tp, d)
    xs = x_sample.reshape(ts, d)
    tm = 256

    proj_p = _in_proj(xp, mod_p[0], mod_p[1], w_in_bf, seq, 512)
    proj_s = _in_proj(xs, mod_s[0], mod_s[1], w_in_bf, 0, ts)
    ys5_p, pre, pim = _s5_prompt(proj_p, n_p, seq, sw, wglu_bf, b_glu)
    ys5_s, sre, sim = _s5_sample(proj_s, state_s5_re[l].reshape(n_s, S5_NSTATE),
                                 state_s5_im[l].reshape(n_s, S5_NSTATE), n_s, steps, sw, wglu_bf, b_glu)
    yhg_p, phg = _hgrn_prompt(proj_p, lb, gain, n_p, seq)
    yhg_s, shg = _hgrn_sample(proj_s, state_hgrn[l], lb, gain, n_s, steps)

    x1p, h2_all, lg_p = _out_proj(ys5_p, yhg_p, xp, mod_p[2], mod_p[3], mod_p[4], wo_s, wo_h, ln1g, ln1b,
                                  wr_hi, wr_lo, seq, tm, 0, None, t_all)
    x1s, h2_all, lg_s = _out_proj(ys5_s, yhg_s, xs, mod_s[2], mod_s[3], mod_s[4], wo_s, wo_h, ln1g, ln1b,
                                  wr_hi, wr_lo, 0, tm, tp, h2_all, t_all)

    idx_t, gate_t = _router(jnp.concatenate([lg_p, lg_s], axis=1), router_bias[l])
    n_blocks = -(-(t_all * TOP_K + N_EXPERTS * (MOE_BLOCK - 1)) // MOE_BLOCK)
    blk_expert, nact, tok, dst = _dispatch(idx_t.T, t_all, n_blocks)
    ybuf = _moe_experts(h2_all, w_exp_gate[l], w_exp_up[l], w_exp_down[l], blk_expert, nact, tok, dst,
                        n_blocks)
    gate = gate_t.T
    wsg, wsu, wsd = w_sh_gate[l].astype(BF16), w_sh_up[l].astype(BF16), w_sh_down[l].astype(BF16)
    tc = 128
    yp = _combine(h2_all, x1p, mod_p[5], gate, ybuf, wsg, wsu, wsd, ln2g, ln2b, seq, tc, 0, t_all)
    ys = _combine(h2_all, x1s, mod_s[5], gate, ybuf, wsg, wsu, wsd, ln2g, ln2b, 0, tc, tp, t_all)

    g, p = S5_GROUPS, S5_STATE
    return (yp.reshape(n_p, seq, d), ys.reshape(n_s, steps, d),
            pre.reshape(1, n_p, g, p), pim.reshape(1, n_p, g, p), phg[None],
            sre.reshape(1, n_s, g, p), sim.reshape(1, n_s, g, p), shg[None])
```
